```python
import math
import jax, jax.numpy as jnp
from jax import lax
import numpy as np

D_MODEL = 1024
BATCH = 8
SEQ = 4096
DEPTH = 4

N_A_LAYERS = DEPTH // 2
N_B_LAYERS = DEPTH - N_A_LAYERS

CHUNK = 128
A_WIDTH = 2 * D_MODEL
A_GROUPS = 8
A_GROUP_DIM = A_WIDTH // A_GROUPS

N_HEADS = 16
N_KV_HEADS = 4
Q_PER_KV = N_HEADS // N_KV_HEADS
HEAD_DIM = 64
WINDOW = 128
BLOCK = 128

N_BUCKETS = 32
MAX_DISTANCE = 128

D_FF = 4 * D_MODEL
EPS = 1e-6

kernel_name = "yoco_sgu_swa_sink_hybrid"


def rms_norm(x, g):
    xf = x.astype(jnp.float32)
    y = xf * lax.rsqrt(jnp.mean(xf * xf, axis=-1, keepdims=True) + EPS)
    return (y * g.astype(jnp.float32)).astype(x.dtype)


def layer_norm(x, g):
    xf = x.astype(jnp.float32)
    mu = jnp.mean(xf, axis=-1, keepdims=True)
    xc = xf - mu
    y = xc * lax.rsqrt(jnp.mean(xc * xc, axis=-1, keepdims=True) + EPS)
    return (y * g.astype(jnp.float32)).astype(x.dtype)


def t5_causal_bucket(dist):
    max_exact = N_BUCKETS // 2
    d = jnp.maximum(dist, 0)
    log_ratio = jnp.log(jnp.maximum(d, 1).astype(jnp.float32) / max_exact) / math.log(MAX_DISTANCE / max_exact)
    large = jnp.minimum(max_exact + (log_ratio * (N_BUCKETS - max_exact)).astype(jnp.int32), N_BUCKETS - 1)
    return jnp.where(d < max_exact, d, large)


def band_bias_and_mask(rel_bias, n_blocks):
    qi = jnp.arange(BLOCK)[:, None]
    kj = jnp.arange(2 * BLOCK)[None, :]
    dist = qi + BLOCK - kj
    in_window = (dist >= 0) & (dist < WINDOW)
    bias = jnp.transpose(rel_bias.astype(jnp.float32)[t5_causal_bucket(dist)], (2, 0, 1))
    k_pos = (jnp.arange(n_blocks)[:, None, None] - 1) * BLOCK + kj[None]
    mask = in_window[None] & (k_pos >= 0)
    return bias, mask


def chunked_sgu_mixer(h, w_in, ln_g, w_s, b_s, w_out):
    B, S, _ = h.shape
    uv = jax.nn.gelu(h @ w_in, approximate=False)
    u, v = jnp.split(uv, 2, axis=-1)
    v = layer_norm(v, ln_g).reshape(B, S // CHUNK, CHUNK, A_GROUPS, A_GROUP_DIM)
    causal = jnp.tril(jnp.ones((CHUNK, CHUNK), dtype=bool))
    w_causal = jnp.where(causal[None], w_s, jnp.zeros_like(w_s))
    mixed = jnp.einsum('gij,bnjgd->bnigd', w_causal, v) + jnp.transpose(b_s)[None, None, :, :, None]
    gated = u * mixed.reshape(B, S, A_WIDTH)
    return gated @ w_out


def shared_band_kv(h, kv_norm_g, w_k, w_v):
    B, S, _ = h.shape
    nb = S // BLOCK
    hn = rms_norm(h, kv_norm_g)
    k = (hn @ w_k).reshape(B, nb, BLOCK, N_KV_HEADS, HEAD_DIM)
    v = (hn @ w_v).reshape(B, nb, BLOCK, N_KV_HEADS, HEAD_DIM)

    def band(t):
        prev = jnp.pad(t, ((0, 0), (1, 0), (0, 0), (0, 0), (0, 0)))[:, :-1]
        return jnp.concatenate([prev, t], axis=2)

    return band(k), band(v)


def sliding_sink_attention(h, k_band, v_band, w_q, sinks, w_o, band_bias, band_mask):
    B, S, _ = h.shape
    nb = S // BLOCK
    q = (h @ w_q).reshape(B, nb, BLOCK, N_KV_HEADS, Q_PER_KV, HEAD_DIM)
    logits = jnp.einsum('bnqkgd,bnjkd->bnkgqj', q, k_band).astype(jnp.float32) * (HEAD_DIM ** -0.5)
    logits = logits + band_bias.reshape(N_KV_HEADS, Q_PER_KV, BLOCK, 2 * BLOCK)
    logits = jnp.where(band_mask[None, :, None, None], logits, jnp.finfo(jnp.float32).min)
    sink = sinks.astype(jnp.float32).reshape(N_KV_HEADS, Q_PER_KV)[None, None, :, :, None, None]
    m = jnp.maximum(jnp.max(logits, axis=-1, keepdims=True), sink)
    e = jnp.exp(logits - m)
    probs = e / (jnp.sum(e, axis=-1, keepdims=True) + jnp.exp(sink - m))
    out = jnp.einsum('bnkgqj,bnjkd->bnqkgd', probs.astype(v_band.dtype), v_band)
    return out.reshape(B, S, N_HEADS * HEAD_DIM) @ w_o


def sq_relu_mlp(h, w1, w2):
    return jnp.square(jax.nn.relu(h @ w1)) @ w2


def setup_inputs(seed: int = 0) -> dict:
    key = jax.random.key(seed)
    ks = jax.random.split(key, 20)
    f32 = jnp.float32

    def nrm(k, shape, scale):
        return jax.random.normal(k, shape, f32) * scale

    return {
        "x": jax.random.normal(ks[0], (BATCH, SEQ, D_MODEL), f32),
        "mix_norm_g": 1.0 + nrm(ks[1], (DEPTH, D_MODEL), 0.05),
        "ffn_norm_g": 1.0 + nrm(ks[2], (DEPTH, D_MODEL), 0.05),
        "a_w_in": nrm(ks[3], (N_A_LAYERS, D_MODEL, 2 * A_WIDTH), D_MODEL ** -0.5),
        "a_ln_g": 1.0 + nrm(ks[4], (N_A_LAYERS, A_WIDTH), 0.05),
        "a_w_spatial": nrm(ks[5], (N_A_LAYERS, A_GROUPS, CHUNK, CHUNK), CHUNK ** -0.5),
        "a_b_spatial": 1.0 + nrm(ks[6], (N_A_LAYERS, A_GROUPS, CHUNK), 0.1),
        "a_w_out": nrm(ks[7], (N_A_LAYERS, A_WIDTH, D_MODEL), A_WIDTH ** -0.5),
        "kv_norm_g": 1.0 + nrm(ks[8], (D_MODEL,), 0.05),
        "w_k": nrm(ks[9], (D_MODEL, N_KV_HEADS * HEAD_DIM), D_MODEL ** -0.5),
        "w_v": nrm(ks[10], (D_MODEL, N_KV_HEADS * HEAD_DIM), D_MODEL ** -0.5),
        "b_w_q": nrm(ks[11], (N_B_LAYERS, D_MODEL, N_HEADS * HEAD_DIM), D_MODEL ** -0.5),
        "b_sinks": nrm(ks[12], (N_B_LAYERS, N_HEADS), 0.5),
        "b_w_o": nrm(ks[13], (N_B_LAYERS, N_HEADS * HEAD_DIM, D_MODEL), (N_HEADS * HEAD_DIM) ** -0.5),
        "rel_bias": nrm(ks[14], (N_BUCKETS, N_HEADS), 0.5),
        "ffn_w1": nrm(ks[15], (DEPTH, D_MODEL, D_FF), D_MODEL ** -0.5),
        "ffn_w2": nrm(ks[16], (DEPTH, D_FF, D_MODEL), 0.5 * D_FF ** -0.5),
        "final_norm_g": 1.0 + nrm(ks[17], (D_MODEL,), 0.05),
    }


def reference(x, mix_norm_g, ffn_norm_g, a_w_in, a_ln_g, a_w_spatial, a_b_spatial, a_w_out,
              kv_norm_g, w_k, w_v, b_w_q, b_sinks, b_w_o, rel_bias, ffn_w1, ffn_w2, final_norm_g):
    _, S, _ = x.shape
    band_bias, band_mask = band_bias_and_mask(rel_bias, S // BLOCK)
    h = x
    k_band = None
    v_band = None
    for layer in range(DEPTH):
        hn = rms_norm(h, mix_norm_g[layer])
        if layer < N_A_LAYERS:
            i = layer
            h = h + chunked_sgu_mixer(hn, a_w_in[i], a_ln_g[i], a_w_spatial[i], a_b_spatial[i], a_w_out[i])
        else:
            if layer == N_A_LAYERS:
                k_band, v_band = shared_band_kv(h, kv_norm_g, w_k, w_v)
                hn = rms_norm(h, mix_norm_g[layer])
            j = layer - N_A_LAYERS
            h = h + sliding_sink_attention(hn, k_band, v_band, b_w_q[j], b_sinks[j], b_w_o[j], band_bias, band_mask)
        h = h + sq_relu_mlp(rms_norm(h, ffn_norm_g[layer]), ffn_w1[layer], ffn_w2[layer])
    return rms_norm(h, final_norm_g)
```

```python
import functools
import math

import jax
import jax.numpy as jnp
import numpy as np
from jax import lax
from jax.experimental import pallas as pl
from jax.experimental.pallas import tpu as pltpu

D_MODEL = 1024
DEPTH = 4
N_A_LAYERS = DEPTH // 2

CHUNK = 128
A_WIDTH = 2 * D_MODEL
A_GROUPS = 8
A_GROUP_DIM = A_WIDTH // A_GROUPS

N_HEADS = 16
N_KV_HEADS = 4
Q_PER_KV = N_HEADS // N_KV_HEADS
HEAD_DIM = 64
WINDOW = 128
BLOCK = 128
KV_WIDTH = N_KV_HEADS * HEAD_DIM

N_BUCKETS = 32
MAX_DISTANCE = 128

D_FF = 4 * D_MODEL
EPS = 1e-6

F32 = jnp.float32
BF16 = jnp.bfloat16
F32_MIN = float(jnp.finfo(jnp.float32).min)

LANES = 128
V7X_VMEM_BYTES = 64 * 1024 * 1024

TOKEN_TILE = 512
FF_CHUNK = 1024

HEAD_ORDER = tuple(
    (2 * p + half) * Q_PER_KV + g
    for p in range(N_KV_HEADS // 2)
    for g in range(Q_PER_KV)
    for half in range(2)
)


def _rms_norm(x, g):
    return x * lax.rsqrt(jnp.mean(x * x, axis=-1, keepdims=True) + EPS) * g


def _gelu(x):
    return 0.5 * x * (1.0 + lax.erf(x * np.float32(math.sqrt(0.5))))


def _resident(shape):
    zeros = (0,) * len(shape)
    return pl.BlockSpec(shape, lambda i: zeros, pipeline_mode=pl.Buffered(1))


def _token_spec(width):
    return pl.BlockSpec((TOKEN_TILE, width), lambda i: (i, 0))


def _params(vmem_bytes):
    return pltpu.CompilerParams(
        dimension_semantics=("parallel",),
        vmem_limit_bytes=min(int(vmem_bytes), V7X_VMEM_BYTES - 4 * 1024 * 1024),
    )


def _sgu_body(h_ref, g_ref, w_in_ref, ln_g_ref, ws_ref, bs_ref, w_out_ref, o_ref, vn_ref):
    h = h_ref[...]
    hn = _rms_norm(h, g_ref[...]).astype(BF16)

    v = _gelu(jnp.dot(hn, w_in_ref[:, A_WIDTH:], preferred_element_type=F32))
    xc = v - jnp.mean(v, axis=-1, keepdims=True)
    vn = xc * lax.rsqrt(jnp.mean(xc * xc, axis=-1, keepdims=True) + EPS) * ln_g_ref[...]
    vn_ref[...] = vn.astype(BF16)

    row = lax.broadcasted_iota(jnp.int32, (CHUNK, CHUNK), 0)
    col = lax.broadcasted_iota(jnp.int32, (CHUNK, CHUNK), 1)
    causal = col <= row

    acc = h
    for g in range(A_GROUPS):
        lo, hi = g * A_GROUP_DIM, (g + 1) * A_GROUP_DIM
        u = _gelu(jnp.dot(hn, w_in_ref[:, lo:hi], preferred_element_type=F32))
        w_causal = jnp.where(causal, ws_ref[g], 0.0).astype(BF16)
        bias = bs_ref[g]
        mixed = jnp.concatenate(
            [
                jnp.dot(w_causal, vn_ref[c * CHUNK:(c + 1) * CHUNK, lo:hi],
                        preferred_element_type=F32) + bias
                for c in range(TOKEN_TILE // CHUNK)
            ],
            axis=0,
        )
        gated = (u * mixed).astype(BF16)
        acc = acc + jnp.dot(gated, w_out_ref[lo:hi, :], preferred_element_type=F32)
    o_ref[...] = acc


def _sgu_layer(h, g, w_in, ln_g, w_s, b_s_wide, w_out):
    tokens = h.shape[0]
    vmem = (
        2 * (w_in.size + w_out.size)
        + 4 * (w_s.size + b_s_wide.size)
        + 4 * 4 * TOKEN_TILE * D_MODEL
        + 2 * TOKEN_TILE * A_WIDTH
        + 6 * 4 * TOKEN_TILE * A_WIDTH
    )
    return pl.pallas_call(
        _sgu_body,
        grid=(tokens // TOKEN_TILE,),
        in_specs=[
            _token_spec(D_MODEL),
            _resident((1, D_MODEL)),
            _resident(w_in.shape),
            _resident((1, A_WIDTH)),
            _resident(w_s.shape),
            _resident(b_s_wide.shape),
            _resident(w_out.shape),
        ],
        out_specs=_token_spec(D_MODEL),
        out_shape=jax.ShapeDtypeStruct(h.shape, F32),
        scratch_shapes=[pltpu.VMEM((TOKEN_TILE, A_WIDTH), BF16)],
        compiler_params=_params(vmem),
        name="sgu_mixer",
    )(h, g, w_in, ln_g, w_s, b_s_wide, w_out)


def _ffn_core(h_ref, g_ref, w1_ref, w2_ref):
    h = h_ref[...]
    hn = _rms_norm(h, g_ref[...]).astype(BF16)
    acc = h
    for c in range(D_FF // FF_CHUNK):
        lo, hi = c * FF_CHUNK, (c + 1) * FF_CHUNK
        a = jnp.dot(hn, w1_ref[:, lo:hi], preferred_element_type=F32)
        r = jnp.square(jnp.maximum(a, 0.0)).astype(BF16)
        acc = acc + jnp.dot(r, w2_ref[lo:hi, :], preferred_element_type=F32)
    return acc


def _ffn_body(h_ref, g_ref, w1_ref, w2_ref, o_ref):
    o_ref[...] = _ffn_core(h_ref, g_ref, w1_ref, w2_ref)


def _ffn_kv_body(h_ref, g_ref, w1_ref, w2_ref, kv_g_ref, w_kv_ref, o_ref, kv_ref):
    h_new = _ffn_core(h_ref, g_ref, w1_ref, w2_ref)
    o_ref[...] = h_new
    hn = _rms_norm(h_new, kv_g_ref[...]).astype(BF16)
    kv_ref[...] = jnp.dot(hn, w_kv_ref[...], preferred_element_type=F32).astype(BF16)


def _ffn_final_body(h_ref, g_ref, w1_ref, w2_ref, final_g_ref, o_ref):
    o_ref[...] = _rms_norm(_ffn_core(h_ref, g_ref, w1_ref, w2_ref), final_g_ref[...])


def _ffn_layer(h, g, w1, w2, *, kv=None, final_g=None):
    tokens = h.shape[0]
    vmem = (
        2 * (w1.size + w2.size)
        + 4 * 4 * TOKEN_TILE * D_MODEL
        + 4 * 4 * TOKEN_TILE * FF_CHUNK
        + 4 * 4 * TOKEN_TILE * D_MODEL
    )
    in_specs = [_token_spec(D_MODEL), _resident((1, D_MODEL)), _resident(w1.shape), _resident(w2.shape)]
    args = [h, g, w1, w2]
    out_specs = _token_spec(D_MODEL)
    out_shape = jax.ShapeDtypeStruct(h.shape, F32)
    if kv is not None:
        kv_g, w_kv = kv
        body = _ffn_kv_body
        in_specs += [_resident((1, D_MODEL)), _resident(w_kv.shape)]
        args += [kv_g, w_kv]
        out_specs = [out_specs, _token_spec(2 * KV_WIDTH)]
        out_shape = [out_shape, jax.ShapeDtypeStruct((tokens, 2 * KV_WIDTH), BF16)]
        vmem += 2 * w_kv.size + 4 * 2 * TOKEN_TILE * 2 * KV_WIDTH
    elif final_g is not None:
        body = _ffn_final_body
        in_specs += [_resident((1, D_MODEL))]
        args += [final_g]
    else:
        body = _ffn_body
    return pl.pallas_call(
        body,
        grid=(tokens // TOKEN_TILE,),
        in_specs=in_specs,
        out_specs=out_specs,
        out_shape=out_shape,
        compiler_params=_params(vmem),
        name="relu2_mlp",
    )(*args)


def _attn_body(tiles_per_seq, sinks_ref, h_ref, kv_ref, kv_prev_ref, g_ref, wq_ref, wo_ref,
               bias_ref, o_ref, att_ref):
    h = h_ref[...]
    hn = _rms_norm(h, g_ref[...]).astype(BF16)
    q = jnp.dot(hn, wq_ref[...], preferred_element_type=F32).astype(BF16)

    qi = lax.broadcasted_iota(jnp.int32, (BLOCK, 2 * BLOCK), 0)
    kj = lax.broadcasted_iota(jnp.int32, (BLOCK, 2 * BLOCK), 1)
    dist = qi + BLOCK - kj
    in_window = (dist >= 0) & (dist < WINDOW)
    seq_start = (pl.program_id(0) % tiles_per_seq) == 0
    first_mask = in_window & ((kj >= BLOCK) | jnp.logical_not(seq_start))
    low_half = lax.broadcasted_iota(jnp.int32, (2 * BLOCK, LANES), 1) < HEAD_DIM

    for b in range(TOKEN_TILE // BLOCK):
        rows = slice(b * BLOCK, (b + 1) * BLOCK)
        if b == 0:
            kv_band = jnp.concatenate([kv_prev_ref[...], kv_ref[rows, :]], axis=0)
            mask = first_mask
        else:
            kv_band = kv_ref[(b - 1) * BLOCK:(b + 1) * BLOCK, :]
            mask = in_window
        for p in range(N_KV_HEADS // 2):
            k_pair = kv_band[:, p * LANES:(p + 1) * LANES]
            v_pair = kv_band[:, KV_WIDTH + p * LANES:KV_WIDTH + (p + 1) * LANES]
            q4 = jnp.concatenate(
                [q[rows, (p * Q_PER_KV + g) * LANES:(p * Q_PER_KV + g + 1) * LANES]
                 for g in range(Q_PER_KV)],
                axis=0,
            )
            out4 = None
            for half in range(2):
                keep = low_half if half == 0 else jnp.logical_not(low_half)
                k_half = jnp.where(keep, k_pair, jnp.zeros_like(k_pair))
                v_half = jnp.where(keep, v_pair, jnp.zeros_like(v_pair))
                s = lax.dot_general(q4, k_half, (((1,), (1,)), ((), ())),
                                    preferred_element_type=F32)
                e_parts, r_parts = [], []
                for g in range(Q_PER_KV):
                    slot = 2 * (p * Q_PER_KV + g) + half
                    sink = sinks_ref[slot]
                    logits = s[g * BLOCK:(g + 1) * BLOCK, :] + bias_ref[slot]
                    logits = jnp.where(mask, logits, F32_MIN)
                    m = jnp.maximum(jnp.max(logits, axis=-1, keepdims=True), sink)
                    e = jnp.exp(logits - m)
                    denom = jnp.sum(e, axis=-1, keepdims=True) + jnp.exp(sink - m)
                    e_parts.append(e.astype(BF16))
                    r_parts.append(1.0 / denom)
                pv = jnp.dot(jnp.concatenate(e_parts, axis=0), v_half,
                             preferred_element_type=F32)
                pv = pv * jnp.concatenate(r_parts, axis=0)
                out4 = pv if out4 is None else out4 + pv
            for g in range(Q_PER_KV):
                t = p * Q_PER_KV + g
                att_ref[rows, t * LANES:(t + 1) * LANES] = out4[g * BLOCK:(g + 1) * BLOCK, :].astype(BF16)

    o_ref[...] = h + jnp.dot(att_ref[...], wo_ref[...], preferred_element_type=F32)


def _attn_layer(h, kv, g, w_q, w_o, sinks, bias, seq_len):
    tokens = h.shape[0]
    blocks_per_tile = TOKEN_TILE // BLOCK
    vmem = (
        2 * (w_q.size + w_o.size)
        + 4 * bias.size
        + 4 * 4 * TOKEN_TILE * D_MODEL
        + 2 * 2 * (TOKEN_TILE + BLOCK) * 2 * KV_WIDTH
        + 2 * TOKEN_TILE * D_MODEL
        + 6 * 4 * TOKEN_TILE * D_MODEL
    )
    return pl.pallas_call(
        functools.partial(_attn_body, seq_len // TOKEN_TILE),
        grid=(tokens // TOKEN_TILE,),
        in_specs=[
            pl.BlockSpec(memory_space=pltpu.SMEM),
            _token_spec(D_MODEL),
            _token_spec(2 * KV_WIDTH),
            pl.BlockSpec((BLOCK, 2 * KV_WIDTH),
                         lambda i: (jnp.maximum(i * blocks_per_tile - 1, 0), 0)),
            _resident((1, D_MODEL)),
            _resident(w_q.shape),
            _resident(w_o.shape),
            _resident(bias.shape),
        ],
        out_specs=_token_spec(D_MODEL),
        out_shape=jax.ShapeDtypeStruct(h.shape, F32),
        scratch_shapes=[pltpu.VMEM((TOKEN_TILE, D_MODEL), BF16)],
        compiler_params=_params(vmem),
        name="swa_sink_attention",
    )(sinks, h, kv, kv, g, w_q, w_o, bias)


def _t5_causal_bucket_table():
    max_exact = N_BUCKETS // 2
    qi = np.arange(BLOCK)[:, None]
    kj = np.arange(2 * BLOCK)[None, :]
    d = np.maximum(qi + BLOCK - kj, 0)
    log_ratio = (np.log(np.maximum(d, 1).astype(np.float32) / np.float32(max_exact))
                 / np.float32(math.log(MAX_DISTANCE / max_exact)))
    large = np.minimum(max_exact + (log_ratio * (N_BUCKETS - max_exact)).astype(np.int32),
                       N_BUCKETS - 1)
    return np.where(d < max_exact, d, large).astype(np.int32)


def _head_columns(w):
    lead = w.shape[0]
    return w.reshape(lead, N_HEADS, HEAD_DIM)[:, np.array(HEAD_ORDER), :].reshape(lead, N_HEADS * HEAD_DIM)


def _head_rows(w):
    trail = w.shape[1]
    return w.reshape(N_HEADS, HEAD_DIM, trail)[np.array(HEAD_ORDER)].reshape(N_HEADS * HEAD_DIM, trail)


def kernel(x, mix_norm_g, ffn_norm_g, a_w_in, a_ln_g, a_w_spatial, a_b_spatial, a_w_out, kv_norm_g, w_k, w_v, b_w_q, b_sinks, b_w_o, rel_bias, ffn_w1, ffn_w2, final_norm_g):
    batch, seq_len, _ = x.shape
    assert seq_len % TOKEN_TILE == 0 and TOKEN_TILE % BLOCK == 0 and TOKEN_TILE % CHUNK == 0
    h = x.reshape(batch * seq_len, D_MODEL)
    row = lambda v: v.reshape(1, -1).astype(F32)
    order = np.array(HEAD_ORDER)

    band_bias = jnp.transpose(rel_bias.astype(F32)[_t5_causal_bucket_table()], (2, 0, 1))[order]
    w_kv = jnp.concatenate([w_k, w_v], axis=1).astype(BF16)

    for layer in range(DEPTH):
        if layer < N_A_LAYERS:
            i = layer
            b_s_wide = jnp.broadcast_to(a_b_spatial[i].astype(F32)[:, :, None],
                                        (A_GROUPS, CHUNK, A_GROUP_DIM))
            h = _sgu_layer(h, row(mix_norm_g[layer]), a_w_in[i].astype(BF16), row(a_ln_g[i]),
                           a_w_spatial[i].astype(F32), b_s_wide, a_w_out[i].astype(BF16))
        else:
            j = layer - N_A_LAYERS
            w_q = _head_columns(b_w_q[j] * (HEAD_DIM ** -0.5)).astype(BF16)
            w_o = _head_rows(b_w_o[j]).astype(BF16)
            h = _attn_layer(h, kv, row(mix_norm_g[layer]), w_q, w_o,
                            b_sinks[j].astype(F32)[order], band_bias, seq_len)
        w1 = ffn_w1[layer].astype(BF16)
        w2 = ffn_w2[layer].astype(BF16)
        g = row(ffn_norm_g[layer])
        if layer == N_A_LAYERS - 1:
            h, kv = _ffn_layer(h, g, w1, w2, kv=(row(kv_norm_g), w_kv))
        elif layer == DEPTH - 1:
            h = _ffn_layer(h, g, w1, w2, final_g=row(final_norm_g))
        else:
            h = _ffn_layer(h, g, w1, w2)
    return h.reshape(batch, seq_len, D_MODEL)
```
